```python
import jax, jax.numpy as jnp
from jax import lax
import numpy as np

D_MODEL = 1024
BATCH = 8
SEQ = 8192
DEPTH = 1

N_META = 16
EPS = 1e-6

SB_HEADS = 8
SB_HEAD_DIM = D_MODEL // 16
SB_WIDTH = SB_HEADS * SB_HEAD_DIM
SB_BLOCK = 128

GLA_HEADS = 4
GLA_DV = D_MODEL // 8
GLA_DK = GLA_DV // 2
GLA_WIDTH = GLA_HEADS * GLA_DV
GLA_KW = GLA_HEADS * GLA_DK
GLA_GATE_RANK = 16
GLA_TAU = 16.0
GLA_CHUNK = 64

MIX_WIDTH = SB_WIDTH + GLA_WIDTH
IN_SPLITS = [SB_WIDTH, SB_WIDTH, SB_WIDTH, GLA_KW, GLA_KW, GLA_WIDTH, GLA_WIDTH, GLA_GATE_RANK]
IN_COLS = sum(IN_SPLITS)

PEER_HEADS = 8
PEER_NKEYS = 128
PEER_TOPK = 16
PEER_QDIM = 128
PEER_HALF = PEER_QDIM // 2
N_EXPERTS = PEER_NKEYS * PEER_NKEYS
PEER_BLOCK = 256

kernel_name = "hymba_sb_gla_peer_block"


def rmsnorm(x, g):
    xf = x.astype(jnp.float32)
    y = xf * lax.rsqrt(jnp.mean(xf * xf, axis=-1, keepdims=True) + EPS)
    return (y * g.astype(jnp.float32)).astype(x.dtype)


def split_heads(t, n):
    b, l, _ = t.shape
    return t.reshape(b, l, n, -1).transpose(0, 2, 1, 3)


def merge_heads(t):
    b, h, l, d = t.shape
    return t.transpose(0, 2, 1, 3).reshape(b, l, h * d)


def sb_query_block(q_blk, q_pos, k, v):
    L = k.shape[2]
    z = jnp.einsum('bhqd,bhkd->bhqk', q_blk, k).astype(jnp.float32) * (SB_HEAD_DIM ** -0.5)
    mask = jnp.arange(L)[None, :] < q_pos[:, None]
    log_1m = jnp.where(mask, jax.nn.log_sigmoid(-z), 0.0)
    between = lax.cumsum(log_1m, axis=3, reverse=True) - log_1m
    a = jnp.where(mask, jnp.exp(jax.nn.log_sigmoid(z) + between), 0.0)
    return jnp.einsum('bhqk,bhkd->bhqd', a, v.astype(jnp.float32)).astype(v.dtype)


def stick_breaking_attention(q, k, v):
    b, h, L, d = q.shape
    n_real = L - N_META
    o_meta = sb_query_block(q[:, :, :N_META], jnp.arange(N_META), k, v)
    starts = N_META + jnp.arange(n_real // SB_BLOCK) * SB_BLOCK

    def body(start):
        qb = lax.dynamic_slice_in_dim(q, start, SB_BLOCK, axis=2)
        return sb_query_block(qb, start + jnp.arange(SB_BLOCK), k, v)

    o_real = lax.map(body, starts)
    o_real = jnp.moveaxis(o_real, 0, 2).reshape(b, h, n_real, d)
    return jnp.concatenate([o_meta, o_real], axis=2)


def gla_chunk(state, xs):
    q, k, v, lg = xs
    C = q.shape[2]
    bcum = jnp.cumsum(lg, axis=2)
    o_inter = jnp.einsum('bhtd,bhde->bhte', q * jnp.exp(bcum), state)
    causal = (jnp.arange(C)[:, None] >= jnp.arange(C)[None, :])[:, :, None]
    diff = bcum[:, :, :, None, :] - bcum[:, :, None, :, :]
    decay = jnp.where(causal, jnp.exp(jnp.where(causal, diff, 0.0)), 0.0)
    attn = jnp.einsum('bhtd,bhsd,bhtsd->bhts', q, k, decay)
    o = o_inter + jnp.einsum('bhts,bhse->bhte', attn, v)
    b_last = bcum[:, :, -1:, :]
    new_state = jnp.exp(b_last[:, :, 0, :])[..., None] * state + \
        jnp.einsum('bhsd,bhse->bhde', k * jnp.exp(b_last - bcum), v)
    return new_state, o


def gated_linear_attention(q, k, v, lg):
    out_dtype = v.dtype
    q, k, v, lg = (t.astype(jnp.float32) for t in (q, k, v, lg))
    b, h, L, dk = q.shape
    dv = v.shape[-1]
    n_real = L - N_META
    n_chunks = n_real // GLA_CHUNK
    state0 = jnp.zeros((b, h, dk, dv), jnp.float32)
    state, o_meta = gla_chunk(state0, (q[:, :, :N_META], k[:, :, :N_META], v[:, :, :N_META], lg[:, :, :N_META]))

    def to_chunks(t):
        return jnp.moveaxis(t[:, :, N_META:].reshape(b, h, n_chunks, GLA_CHUNK, t.shape[-1]), 2, 0)

    _, o_real = lax.scan(gla_chunk, state, (to_chunks(q), to_chunks(k), to_chunks(v), to_chunks(lg)))
    o_real = jnp.moveaxis(o_real, 0, 2).reshape(b, h, n_real, dv)
    return jnp.concatenate([o_meta, o_real], axis=2).astype(out_dtype)


def hybrid_mixer(hn, w_in, w_gate_up, b_gate_up, sb_norm_g, gla_norm_g, w_out):
    proj = hn @ w_in
    offsets = list(np.cumsum(IN_SPLITS)[:-1])
    sbq, sbk, sbv, gq, gk, gv, gr, glr = jnp.split(proj, offsets, axis=-1)
    sb_o = stick_breaking_attention(split_heads(sbq, SB_HEADS), split_heads(sbk, SB_HEADS),
                                    split_heads(sbv, SB_HEADS))
    sb_o = merge_heads(rmsnorm(sb_o, sb_norm_g.reshape(SB_HEADS, 1, SB_HEAD_DIM)))
    lg = jax.nn.log_sigmoid((glr @ w_gate_up + b_gate_up).astype(jnp.float32)) / GLA_TAU
    gla_o = gated_linear_attention(split_heads(gq, GLA_HEADS) * (GLA_DK ** -0.5),
                                   split_heads(gk, GLA_HEADS), split_heads(gv, GLA_HEADS),
                                   split_heads(lg, GLA_HEADS))
    gla_o = merge_heads(rmsnorm(gla_o, gla_norm_g.reshape(GLA_HEADS, 1, GLA_DV))) * jax.nn.silu(gr)
    return jnp.concatenate([sb_o, gla_o], axis=-1) @ w_out


def peer_ffn(hn, w_peer_q, sub_keys1, sub_keys2, expert_u, expert_v):
    b, L, d = hn.shape
    t = hn.reshape(b * L, d)
    T = t.shape[0]
    q = (t @ w_peer_q).reshape(T, PEER_HEADS, 2, PEER_HALF)
    s1 = jnp.einsum('thd,hkd->thk', q[:, :, 0], sub_keys1).astype(jnp.float32)
    s2 = jnp.einsum('thd,hkd->thk', q[:, :, 1], sub_keys2).astype(jnp.float32)
    v1, i1 = lax.top_k(s1, PEER_TOPK)
    v2, i2 = lax.top_k(s2, PEER_TOPK)
    cand = (v1[..., :, None] + v2[..., None, :]).reshape(T, PEER_HEADS, PEER_TOPK * PEER_TOPK)
    cidx = (i1[..., :, None] * PEER_NKEYS + i2[..., None, :]).reshape(T, PEER_HEADS, PEER_TOPK * PEER_TOPK)
    sc, pos = lax.top_k(cand, PEER_TOPK)
    eidx = jnp.take_along_axis(cidx, pos, axis=-1).reshape(T, PEER_HEADS * PEER_TOPK)
    gates = jax.nn.softmax(sc, axis=-1).reshape(T, PEER_HEADS * PEER_TOPK).astype(hn.dtype)
    n_blocks = -(-T // PEER_BLOCK)
    pad = n_blocks * PEER_BLOCK - T
    tb = jnp.pad(t, ((0, pad), (0, 0))).reshape(n_blocks, PEER_BLOCK, d)
    eb = jnp.pad(eidx, ((0, pad), (0, 0))).reshape(n_blocks, PEER_BLOCK, -1)
    gb = jnp.pad(gates, ((0, pad), (0, 0))).reshape(n_blocks, PEER_BLOCK, -1)

    def block(args):
        xt, e, g = args
        act = jax.nn.gelu(jnp.einsum('nkd,nd->nk', expert_u[e], xt), approximate=False)
        return jnp.einsum('nk,nkd->nd', g * act, expert_v[e])

    y = lax.map(block, (tb, eb, gb)).reshape(n_blocks * PEER_BLOCK, d)[:T]
    return y.reshape(b, L, d)


def setup_inputs(seed: int = 0) -> dict:
    key = jax.random.key(seed)
    ks = jax.random.split(key, 20)
    n = jax.random.normal
    f = jnp.float32
    return {
        "x": n(ks[0], (BATCH, SEQ, D_MODEL), f),
        "meta_tokens": n(ks[1], (N_META, D_MODEL), f),
        "norm1_g": 1.0 + 0.02 * n(ks[2], (DEPTH, D_MODEL), f),
        "w_in": n(ks[3], (DEPTH, D_MODEL, IN_COLS), f) * D_MODEL ** -0.5,
        "w_gate_up": n(ks[4], (DEPTH, GLA_GATE_RANK, GLA_KW), f) * GLA_GATE_RANK ** -0.5,
        "b_gate_up": 0.1 * n(ks[5], (DEPTH, GLA_KW), f),
        "sb_norm_g": 1.0 + 0.02 * n(ks[6], (DEPTH, SB_WIDTH), f),
        "gla_norm_g": 1.0 + 0.02 * n(ks[7], (DEPTH, GLA_WIDTH), f),
        "w_out": n(ks[8], (DEPTH, MIX_WIDTH, D_MODEL), f) * MIX_WIDTH ** -0.5,
        "norm2_g": 1.0 + 0.02 * n(ks[9], (DEPTH, D_MODEL), f),
        "w_peer_q": n(ks[10], (DEPTH, D_MODEL, PEER_HEADS * PEER_QDIM), f) * D_MODEL ** -0.5,
        "sub_keys1": n(ks[11], (DEPTH, PEER_HEADS, PEER_NKEYS, PEER_HALF), f) * PEER_HALF ** -0.5,
        "sub_keys2": n(ks[12], (DEPTH, PEER_HEADS, PEER_NKEYS, PEER_HALF), f) * PEER_HALF ** -0.5,
        "expert_u": n(ks[13], (DEPTH, N_EXPERTS, D_MODEL), f) * D_MODEL ** -0.5,
        "expert_v": n(ks[14], (DEPTH, N_EXPERTS, D_MODEL), f) * PEER_HEADS ** -0.5,
        "norm_f_g": 1.0 + 0.02 * n(ks[15], (D_MODEL,), f),
    }


def reference(x, meta_tokens, norm1_g, w_in, w_gate_up, b_gate_up, sb_norm_g, gla_norm_g, w_out,
              norm2_g, w_peer_q, sub_keys1, sub_keys2, expert_u, expert_v, norm_f_g):
    b = x.shape[0]
    meta = jnp.broadcast_to(meta_tokens.astype(x.dtype)[None], (b, N_META, x.shape[-1]))
    h = jnp.concatenate([meta, x], axis=1)
    for l in range(DEPTH):
        h = h + hybrid_mixer(rmsnorm(h, norm1_g[l]), w_in[l], w_gate_up[l], b_gate_up[l],
                             sb_norm_g[l], gla_norm_g[l], w_out[l])
        h = h + peer_ffn(rmsnorm(h, norm2_g[l]), w_peer_q[l], sub_keys1[l], sub_keys2[l],
                         expert_u[l], expert_v[l])
    return rmsnorm(h, norm_f_g)[:, N_META:]
```

```python
import functools

import numpy as np
import jax
import jax.numpy as jnp
from jax import lax
from jax.experimental import pallas as pl
from jax.experimental.pallas import tpu as pltpu

F32 = jnp.float32
BF16 = jnp.bfloat16

D_MODEL = 1024
N_META = 16
EPS = 1e-6
TILE = 128
PAD = TILE - N_META

SB_HEADS = 8
SB_DH = 64
SB_W = SB_HEADS * SB_DH
GLA_HEADS = 4
GLA_DK = 64
GLA_DV = 128
GLA_KW = GLA_HEADS * GLA_DK
GLA_W = GLA_HEADS * GLA_DV
GATE_RANK = 16
GLA_TAU = 16.0
GLA_SUB = 16

PEER_HEADS = 8
PEER_NKEYS = 128
PEER_TOPK = 16
PEER_HALF = 64
PEER_K = PEER_HEADS * PEER_TOPK

VMEM_LIMIT = 56 * 1024 * 1024

NT_DIMS = (((1,), (1,)), ((), ()))


def _largest_tile(n_tiles, cap):
    best = 1
    for d in range(1, cap + 1):
        if n_tiles % d == 0:
            best = d
    return best


def _pad_sequence(x, meta_tokens):
    B, _, D = x.shape
    meta = jnp.broadcast_to(meta_tokens.astype(x.dtype)[None], (B, N_META, D))
    return jnp.concatenate([jnp.zeros((B, PAD, D), x.dtype), meta, x], axis=1)


def _inproj_kernel(h_ref, g1_ref, wqT_ref, wk_ref, wvT_ref, wg_ref, wglr_ref, wgu_ref, bgu_ref,
                   qT_ref, k_ref, vT_ref, gq_ref, gk_ref, gv_ref, gr_ref, lg_ref):
    x = h_ref[...]
    ms = jnp.mean(x * x, axis=-1, keepdims=True)
    hn = (x * lax.rsqrt(ms + EPS) * g1_ref[...]).astype(BF16)
    qT_ref[...] = lax.dot_general(wqT_ref[...], hn, NT_DIMS, preferred_element_type=F32).astype(BF16)
    k_ref[...] = jnp.dot(hn, wk_ref[...], preferred_element_type=F32).astype(BF16)
    vT_ref[...] = lax.dot_general(wvT_ref[...], hn, NT_DIMS, preferred_element_type=F32).astype(BF16)
    g = jnp.dot(hn, wg_ref[...], preferred_element_type=F32)
    gq_ref[...] = g[:, 0:GLA_KW]
    gk_ref[...] = g[:, GLA_KW:2 * GLA_KW]
    gv_ref[...] = g[:, 2 * GLA_KW:2 * GLA_KW + GLA_W].astype(BF16)
    gr_ref[...] = g[:, 2 * GLA_KW + GLA_W:]
    glr = jnp.dot(hn, wglr_ref[...], preferred_element_type=F32)
    pre = jnp.dot(glr.astype(BF16), wgu_ref[...], preferred_element_type=F32) + bgu_ref[...]
    lg_ref[...] = (jnp.minimum(pre, 0.0) - jnp.log1p(jnp.exp(-jnp.abs(pre)))) * (1.0 / GLA_TAU)


def _inproj(h_pad, g1, w_in, w_gate_up, b_gate_up):
    B, Lp, D = h_pad.shape
    tm = TILE * _largest_tile(Lp // TILE, 5)
    o = np.cumsum([0, SB_W, SB_W, SB_W, GLA_KW, GLA_KW, GLA_W, GLA_W, GATE_RANK])
    wqT = (w_in[:, o[0]:o[1]] * (SB_DH ** -0.5)).T.astype(BF16)
    wk = w_in[:, o[1]:o[2]].astype(BF16)
    wvT = w_in[:, o[2]:o[3]].T.astype(BF16)
    wg = jnp.concatenate([w_in[:, o[3]:o[4]] * (GLA_DK ** -0.5), w_in[:, o[4]:o[7]]], axis=1).astype(BF16)
    wglr = jnp.pad(w_in[:, o[7]:o[8]], ((0, 0), (0, TILE - GATE_RANK))).astype(BF16)
    wgu = jnp.pad(w_gate_up, ((0, TILE - GATE_RANK), (0, 0))).astype(BF16)
    bgu = b_gate_up.reshape(1, GLA_KW).astype(F32)
    full = lambda shp: pl.BlockSpec(shp, lambda b, i: (0,) * len(shp))
    row = lambda w: pl.BlockSpec((None, tm, w), lambda b, i: (b, i, 0))
    col = lambda w: pl.BlockSpec((None, w, tm), lambda b, i: (b, 0, i))
    out_shape = (
        jax.ShapeDtypeStruct((B, SB_W, Lp), BF16),
        jax.ShapeDtypeStruct((B, Lp, SB_W), BF16),
        jax.ShapeDtypeStruct((B, SB_W, Lp), BF16),
        jax.ShapeDtypeStruct((B, Lp, GLA_KW), F32),
        jax.ShapeDtypeStruct((B, Lp, GLA_KW), F32),
        jax.ShapeDtypeStruct((B, Lp, GLA_W), BF16),
        jax.ShapeDtypeStruct((B, Lp, GLA_W), F32),
        jax.ShapeDtypeStruct((B, Lp, GLA_KW), F32),
    )
    return pl.pallas_call(
        _inproj_kernel,
        out_shape=out_shape,
        grid=(B, Lp // tm),
        in_specs=[row(D), full((1, D)), full((SB_W, D)), full((D, SB_W)), full((SB_W, D)),
                  full((D, 2 * GLA_KW + 2 * GLA_W)), full((D, TILE)), full((TILE, GLA_KW)), full((1, GLA_KW))],
        out_specs=(col(SB_W), row(SB_W), col(SB_W), row(GLA_KW), row(GLA_KW), row(GLA_W), row(GLA_W), row(GLA_KW)),
        compiler_params=pltpu.CompilerParams(dimension_semantics=("parallel", "parallel"),
                                             vmem_limit_bytes=VMEM_LIMIT),
        name="inproj",
    )(h_pad, g1.reshape(1, D).astype(F32), wqT, wk, wvT, wg, wglr, wgu, bgu)


def _sb_kernel(qT_ref, k_ref, vT_ref, u2_ref, g_ref, oT_ref):
    h = pl.program_id(1)
    i = pl.program_id(2)
    odd = (h % 2) == 1
    qT = qT_ref[...]
    zero = jnp.zeros_like(qT)
    qpad = jnp.concatenate([jnp.where(odd, zero, qT), jnp.where(odd, qT, zero)], axis=0)
    u2 = u2_ref[...]
    srow = lax.broadcasted_iota(jnp.int32, (TILE, TILE), 0)
    tcol = lax.broadcasted_iota(jnp.int32, (TILE, TILE), 1) + i * TILE

    def body(step, carry):
        oT, c = carry
        j = i - step
        ks = pl.multiple_of(j * TILE, TILE)
        kj = k_ref[pl.ds(ks, TILE), :]
        z = jnp.dot(kj, qpad, preferred_element_type=F32)
        s = srow + j * TILE
        valid = (s < tcol) & (s >= PAD)
        sp = jnp.maximum(z, 0.0) + jnp.log(1.0 + jnp.exp(-jnp.abs(z)))
        spm = jnp.where(valid, sp, 0.0)
        hi = spm.astype(BF16)
        lo = (spm - hi.astype(F32)).astype(BF16)
        gr = jnp.dot(u2, jnp.concatenate([hi, lo], axis=0), preferred_element_type=F32)
        w = z - gr[0:TILE, :] + c
        a = jnp.where(valid, jnp.exp(w), 0.0)
        vj = vT_ref[:, pl.ds(ks, TILE)]
        oT = oT + jnp.dot(vj, a.astype(BF16), preferred_element_type=F32)
        c = c - gr[TILE:TILE + 1, :]
        return oT, c

    oT, _ = lax.fori_loop(0, i + 1, body, (jnp.zeros((SB_DH, TILE), F32), jnp.zeros((1, TILE), F32)))
    ms = jnp.mean(oT * oT, axis=0, keepdims=True)
    oT_ref[...] = oT * lax.rsqrt(ms + EPS) * g_ref[...]


def _sb_attention(qT, k, vT, sb_norm_g):
    B, _, Lp = qT.shape
    nq = Lp // TILE
    u = np.triu(np.ones((TILE, TILE), np.float32))
    u2 = np.concatenate([np.concatenate([u, u], axis=1), np.ones((8, 2 * TILE), np.float32)], axis=0)
    gcol = jnp.broadcast_to(sb_norm_g.reshape(SB_W, 1).astype(F32), (SB_W, TILE))
    return pl.pallas_call(
        _sb_kernel,
        out_shape=jax.ShapeDtypeStruct((B, SB_W, Lp), F32),
        grid=(B, SB_HEADS, nq),
        in_specs=[
            pl.BlockSpec((None, SB_DH, TILE), lambda b, h, i: (b, h, i)),
            pl.BlockSpec((None, Lp, TILE), lambda b, h, i: (b, 0, h // 2)),
            pl.BlockSpec((None, SB_DH, Lp), lambda b, h, i: (b, h, 0)),
            pl.BlockSpec((TILE + 8, 2 * TILE), lambda b, h, i: (0, 0)),
            pl.BlockSpec((SB_DH, TILE), lambda b, h, i: (h, 0)),
        ],
        out_specs=pl.BlockSpec((None, SB_DH, TILE), lambda b, h, i: (b, h, i)),
        compiler_params=pltpu.CompilerParams(dimension_semantics=("parallel", "parallel", "arbitrary"),
                                             vmem_limit_bytes=VMEM_LIMIT),
        name="sb_attention",
    )(qT, k, vT, jnp.asarray(u2, BF16), gcol)


def _gla_kernel(q_ref, k_ref, v_ref, gr_ref, lg_ref, ltc_ref, dsel_ref, gn_ref, o_ref,
                s_scr, bc_scr, attn_scr, plhs_scr):
    c = pl.program_id(1)

    @pl.when(c == 0)
    def _():
        s_scr[...] = jnp.zeros_like(s_scr)

    nsub = TILE // GLA_SUB
    q = q_ref[...]
    k = k_ref[...]
    v = v_ref[...]
    lg = lg_ref[...]
    hi = lg.astype(BF16)
    lo = (lg - hi.astype(F32)).astype(BF16)
    bc_scr[...] = jnp.dot(ltc_ref[...], jnp.concatenate([hi, lo], axis=0), preferred_element_type=F32)
    bcum = bc_scr[0:TILE, :]
    rfull = bc_scr[TILE:2 * TILE, :]
    lane_head = lax.broadcasted_iota(jnp.int32, (1, GLA_KW), 1) // GLA_DK
    hmask = [lane_head == hd for hd in range(GLA_HEADS)]
    rows = lax.broadcasted_iota(jnp.int32, (TILE, GLA_KW), 0)

    qrel = q * jnp.exp(bcum - rfull)
    for hd in range(GLA_HEADS):
        attn_scr[hd, 0:GLA_SUB, :] = jnp.zeros((GLA_SUB, TILE), F32)
    for blk in range(1, nsub):
        r0 = blk * GLA_SUB
        r_b = bc_scr[TILE + r0:TILE + r0 + 1, :]
        ktil = jnp.where(rows < r0, k * jnp.exp(jnp.minimum(r_b - bcum, 0.0)), 0.0).astype(BF16)
        qs = qrel[r0:r0 + GLA_SUB, :]
        qstack = jnp.concatenate([jnp.where(hmask[hd], qs, 0.0) for hd in range(GLA_HEADS)], axis=0).astype(BF16)
        strip = lax.dot_general(qstack, ktil, NT_DIMS, preferred_element_type=F32)
        for hd in range(GLA_HEADS):
            attn_scr[hd, r0:r0 + GLA_SUB, :] = strip[hd * GLA_SUB:(hd + 1) * GLA_SUB, :]

    tl = lax.broadcasted_iota(jnp.int32, (GLA_SUB, GLA_KW), 0)
    for s in range(GLA_SUB):
        parts = []
        for blk in range(nsub):
            r0 = blk * GLA_SUB
            kb = k_ref[r0 + s:r0 + s + 1, :]
            bb = bc_scr[r0 + s:r0 + s + 1, :]
            ok = tl >= s
            dec = jnp.exp(jnp.where(ok, bcum[r0:r0 + GLA_SUB, :] - bb, 0.0))
            parts.append(jnp.where(ok, q[r0:r0 + GLA_SUB, :] * kb * dec, 0.0))
        plhs_scr[:, s * GLA_KW:(s + 1) * GLA_KW] = jnp.concatenate(parts, axis=0).astype(BF16)
    diag = jnp.dot(plhs_scr[...], dsel_ref[...], preferred_element_type=F32)
    trow = lax.broadcasted_iota(jnp.int32, (TILE, GLA_HEADS * TILE), 0) // GLA_SUB
    tcol = (lax.broadcasted_iota(jnp.int32, (TILE, GLA_HEADS * TILE), 1) % TILE) // GLA_SUB
    diag = jnp.where(trow == tcol, diag, 0.0)

    qint = q * jnp.exp(bcum)
    s_old = s_scr[...].astype(BF16)
    gn = gn_ref[...]
    for hd in range(GLA_HEADS):
        cs = slice(hd * GLA_DV, (hd + 1) * GLA_DV)
        attn = (attn_scr[hd] + diag[:, hd * TILE:(hd + 1) * TILE]).astype(BF16)
        o = jnp.dot(attn, v[:, cs], preferred_element_type=F32)
        o = o + jnp.dot(jnp.where(hmask[hd], qint, 0.0).astype(BF16), s_old, preferred_element_type=F32)
        ms = jnp.mean(o * o, axis=-1, keepdims=True)
        g = gr_ref[:, cs]
        o_ref[:, cs] = (o * lax.rsqrt(ms + EPS) * gn[:, cs] * (g / (1.0 + jnp.exp(-g)))).astype(o_ref.dtype)

    b_last = bc_scr[TILE - 1:TILE, :]
    kdec_t = (k * jnp.exp(b_last - bcum)).T.astype(BF16)
    ds = jnp.dot(kdec_t, v, preferred_element_type=F32)
    dec_t = jnp.broadcast_to(jnp.exp(b_last), (TILE, GLA_KW)).T
    for hd in range(GLA_HEADS):
        rs = slice(hd * GLA_DK, (hd + 1) * GLA_DK)
        s_scr[rs, :] = dec_t[rs, :] * s_scr[rs, :] + ds[rs, hd * GLA_DV:(hd + 1) * GLA_DV]


def _gla(gq, gk, gv, gr, lg, gla_norm_g):
    B, Lp, _ = gq.shape
    nsub = TILE // GLA_SUB
    t = np.arange(TILE)
    lt = (t[None, :] <= t[:, None]).astype(np.float32)
    ltsel = (t[None, :] < (t[:, None] // GLA_SUB) * GLA_SUB).astype(np.float32)
    ltc = np.concatenate([np.concatenate([lt, lt], axis=1), np.concatenate([ltsel, ltsel], axis=1)], axis=0)
    dsel = np.zeros((GLA_SUB, GLA_HEADS, GLA_DK, GLA_HEADS, nsub, GLA_SUB), np.float32)
    for s in range(GLA_SUB):
        for hd in range(GLA_HEADS):
            dsel[s, hd, :, hd, :, s] = 1.0
    dsel = dsel.reshape(GLA_SUB * GLA_KW, GLA_HEADS * TILE)
    row = lambda w: pl.BlockSpec((None, TILE, w), lambda b, c: (b, c, 0))
    full = lambda shp: pl.BlockSpec(shp, lambda b, c: (0,) * len(shp))
    return pl.pallas_call(
        _gla_kernel,
        out_shape=jax.ShapeDtypeStruct((B, Lp, GLA_W), BF16),
        grid=(B, Lp // TILE),
        in_specs=[row(GLA_KW), row(GLA_KW), row(GLA_W), row(GLA_W), row(GLA_KW),
                  full((2 * TILE, 2 * TILE)), full((GLA_SUB * GLA_KW, GLA_HEADS * TILE)), full((1, GLA_W))],
        out_specs=row(GLA_W),
        scratch_shapes=[pltpu.VMEM((GLA_KW, GLA_DV), F32), pltpu.VMEM((2 * TILE, GLA_KW), F32),
                        pltpu.VMEM((GLA_HEADS, TILE, TILE), F32), pltpu.VMEM((TILE, GLA_SUB * GLA_KW), BF16)],
        compiler_params=pltpu.CompilerParams(dimension_semantics=("parallel", "arbitrary"),
                                             vmem_limit_bytes=VMEM_LIMIT),
        name="gla",
    )(gq, gk, gv, gr, lg, jnp.asarray(ltc, BF16), jnp.asarray(dsel, BF16), gla_norm_g.reshape(1, GLA_W).astype(F32))


def _outproj_kernel(sbT_ref, gla_ref, h_ref, w_ref, g2_ref, h2_ref, hn2_ref):
    sb = sbT_ref[...].T.astype(BF16)
    y = jnp.dot(sb, w_ref[0:SB_W, :], preferred_element_type=F32)
    y = y + jnp.dot(gla_ref[...], w_ref[SB_W:SB_W + GLA_W, :], preferred_element_type=F32)
    h2 = h_ref[...] + y
    h2_ref[...] = h2
    ms = jnp.mean(h2 * h2, axis=-1, keepdims=True)
    hn2_ref[...] = h2 * lax.rsqrt(ms + EPS) * g2_ref[...]


def _outproj(sb_oT, gla_o, h_pad, w_out, g2):
    B, Lp, D = h_pad.shape
    tm = TILE * _largest_tile(Lp // TILE, 5)
    row = lambda w: pl.BlockSpec((None, tm, w), lambda b, i: (b, i, 0))
    full = lambda shp: pl.BlockSpec(shp, lambda b, i: (0,) * len(shp))
    return pl.pallas_call(
        _outproj_kernel,
        out_shape=(jax.ShapeDtypeStruct((B, Lp, D), F32), jax.ShapeDtypeStruct((B, Lp, D), F32)),
        grid=(B, Lp // tm),
        in_specs=[pl.BlockSpec((None, SB_W, tm), lambda b, i: (b, 0, i)), row(GLA_W), row(D),
                  full((SB_W + GLA_W, D)), full((1, D))],
        out_specs=(row(D), row(D)),
        compiler_params=pltpu.CompilerParams(dimension_semantics=("parallel", "parallel"),
                                             vmem_limit_bytes=VMEM_LIMIT),
        name="outproj",
    )(sb_oT, gla_o, h_pad, w_out.astype(BF16), g2.reshape(1, D).astype(F32))


ROUTE_TT = 256


def _top16_rows(s, n_rows, val_scr, idx_scr):
    riota = lax.broadcasted_iota(jnp.int32, s.shape, 0)
    for r in range(PEER_TOPK):
        m = jnp.max(s, axis=0, keepdims=True)
        idx = jnp.min(jnp.where(s == m, riota, n_rows), axis=0, keepdims=True)
        val_scr[r:r + 1, :] = m
        idx_scr[r:r + 1, :] = idx
        s = jnp.where(riota == idx, -jnp.inf, s)


def _route_kernel(x_ref, wqT_ref, sk1_ref, sk2_ref, gT_ref, eT_ref,
                  qT_scr, v1_scr, i1_scr, v2_scr, i2_scr, cand_scr, sc_scr, pos_scr):
    tt = x_ref.shape[0]
    qT_scr[...] = lax.dot_general(wqT_ref[...], x_ref[...].astype(BF16), NT_DIMS,
                                  preferred_element_type=F32).astype(BF16)
    k16 = lax.broadcasted_iota(jnp.int32, (PEER_TOPK, TILE), 0)

    def head_body(hd, carry):
        r1 = pl.multiple_of(hd * 2 * PEER_HALF, 2 * PEER_HALF)
        r2 = pl.multiple_of(hd * 2 * PEER_HALF + PEER_HALF, PEER_HALF)
        ro = pl.multiple_of(hd * PEER_TOPK, PEER_TOPK)
        for lb in range(tt // TILE):
            ls = slice(lb * TILE, (lb + 1) * TILE)
            s1 = jnp.dot(sk1_ref[hd], qT_scr[pl.ds(r1, PEER_HALF), ls], preferred_element_type=F32)
            s2 = jnp.dot(sk2_ref[hd], qT_scr[pl.ds(r2, PEER_HALF), ls], preferred_element_type=F32)
            _top16_rows(s1, PEER_NKEYS, v1_scr, i1_scr)
            _top16_rows(s2, PEER_NKEYS, v2_scr, i2_scr)
            v2 = v2_scr[...]
            for a in range(PEER_TOPK):
                cand_scr[a * PEER_TOPK:(a + 1) * PEER_TOPK, :] = v1_scr[a:a + 1, :] + v2
            _top16_rows(cand_scr[...], PEER_TOPK * PEER_TOPK, sc_scr, pos_scr)
            pos = pos_scr[...]
            i1 = i1_scr[...]
            i2 = i2_scr[...]
            e_rows = []
            for r in range(PEER_TOPK):
                pa = pos[r:r + 1, :] // PEER_TOPK
                pb = pos[r:r + 1, :] % PEER_TOPK
                e1 = jnp.sum(jnp.where(k16 == pa, i1, 0), axis=0, keepdims=True)
                e2 = jnp.sum(jnp.where(k16 == pb, i2, 0), axis=0, keepdims=True)
                e_rows.append(e1 * PEER_NKEYS + e2)
            sc = sc_scr[...]
            ex = jnp.exp(sc - sc[0:1, :])
            gT_ref[pl.ds(ro, PEER_TOPK), ls] = ex / jnp.sum(ex, axis=0, keepdims=True)
            eT_ref[pl.ds(ro, PEER_TOPK), ls] = jnp.concatenate(e_rows, axis=0)
        return carry

    lax.fori_loop(0, PEER_HEADS, head_body, 0)


def _peer_route(hn2, w_peer_q, sub_keys1, sub_keys2):
    Tp, D = hn2.shape
    tt = ROUTE_TT
    full = lambda shp: pl.BlockSpec(shp, lambda i: (0,) * len(shp))
    sm = lambda dt: pltpu.VMEM((PEER_TOPK, TILE), dt)
    return pl.pallas_call(
        _route_kernel,
        out_shape=(jax.ShapeDtypeStruct((PEER_K, Tp), F32), jax.ShapeDtypeStruct((PEER_K, Tp), jnp.int32)),
        grid=(Tp // tt,),
        in_specs=[pl.BlockSpec((tt, D), lambda i: (i, 0)), full((D, D)),
                  full((PEER_HEADS, PEER_NKEYS, PEER_HALF)), full((PEER_HEADS, PEER_NKEYS, PEER_HALF))],
        out_specs=(pl.BlockSpec((PEER_K, tt), lambda i: (0, i)), pl.BlockSpec((PEER_K, tt), lambda i: (0, i))),
        scratch_shapes=[pltpu.VMEM((D, tt), BF16), sm(F32), sm(jnp.int32), sm(F32), sm(jnp.int32),
                        pltpu.VMEM((PEER_TOPK * PEER_TOPK, TILE), F32), sm(F32), sm(jnp.int32)],
        compiler_params=pltpu.CompilerParams(dimension_semantics=("parallel",), vmem_limit_bytes=VMEM_LIMIT),
        name="peer_route",
    )(hn2, w_peer_q.T.astype(BF16), sub_keys1.astype(BF16), sub_keys2.astype(BF16))


EXP_TB = 128
EXP_NBUF = 4


def _experts_kernel(eidx_ref, gates_ref, x_ref, h2_ref, gf_ref, uv_hbm, o_ref, buf, sem):
    tb = x_ref.shape[0]
    d = x_ref.shape[1]

    def issue(t):
        slot = t % EXP_NBUF
        for kk in range(PEER_K):
            e = eidx_ref[t, kk]
            pltpu.make_async_copy(uv_hbm.at[pl.ds(e, 1), :], buf.at[slot, pl.ds(kk, 1), :], sem.at[slot]).start()

    def process(t):
        slot = t % EXP_NBUF
        pltpu.make_async_copy(uv_hbm.at[pl.ds(0, PEER_K), :], buf.at[slot], sem.at[slot]).wait()
        u = buf[slot, :, 0:d].astype(BF16)
        v = buf[slot, :, d:2 * d].astype(BF16)
        x = x_ref[pl.ds(t, 1), :].astype(BF16)
        act = lax.dot_general(x, u, NT_DIMS, preferred_element_type=F32)
        gelu = 0.5 * act * (1.0 + lax.erf(act * np.float32(2.0 ** -0.5)))
        cw = (gates_ref[pl.ds(t, 1), :] * gelu).astype(BF16)
        y = jnp.dot(cw, v, preferred_element_type=F32)
        hh = h2_ref[pl.ds(t, 1), :] + y
        ms = jnp.mean(hh * hh, axis=-1, keepdims=True)
        o_ref[pl.ds(t, 1), :] = hh * lax.rsqrt(ms + EPS) * gf_ref[...]

    def body(tt, carry):
        @pl.when(tt < tb)
        def _():
            issue(tt)

        @pl.when(tt >= EXP_NBUF - 1)
        def _():
            process(tt - (EXP_NBUF - 1))

        return carry

    lax.fori_loop(0, tb + EXP_NBUF - 1, body, 0)


def _peer_experts(eidx, gates, hn2, h2, uv, gf):
    Tp, D = hn2.shape
    tb = EXP_TB
    row = lambda w: pl.BlockSpec((tb, w), lambda i: (i, 0))
    return pl.pallas_call(
        _experts_kernel,
        out_shape=jax.ShapeDtypeStruct((Tp, D), F32),
        grid=(Tp // tb,),
        in_specs=[pl.BlockSpec((tb, PEER_K), lambda i: (i, 0), memory_space=pltpu.SMEM),
                  row(PEER_K), row(D), row(D), pl.BlockSpec((1, D), lambda i: (0, 0)),
                  pl.BlockSpec(memory_space=pl.ANY)],
        out_specs=row(D),
        scratch_shapes=[pltpu.VMEM((EXP_NBUF, PEER_K, 2 * D), F32), pltpu.SemaphoreType.DMA((EXP_NBUF,))],
        compiler_params=pltpu.CompilerParams(dimension_semantics=("arbitrary",), vmem_limit_bytes=VMEM_LIMIT),
        name="peer_experts",
    )(eidx, gates, hn2, h2, gf.reshape(1, D).astype(F32), uv)


def kernel(x, meta_tokens, norm1_g, w_in, w_gate_up, b_gate_up, sb_norm_g, gla_norm_g, w_out,
           norm2_g, w_peer_q, sub_keys1, sub_keys2, expert_u, expert_v, norm_f_g):
    B, S, D = x.shape
    Lp = S + TILE
    h_pad = _pad_sequence(x, meta_tokens)
    qT, k, vT, gq, gk, gv, gr, lg = _inproj(h_pad, norm1_g[0], w_in[0], w_gate_up[0], b_gate_up[0])
    sb_oT = _sb_attention(qT, k, vT, sb_norm_g[0])
    gla_o = _gla(gq, gk, gv, gr, lg, gla_norm_g[0])
    h2, hn2 = _outproj(sb_oT, gla_o, h_pad, w_out[0], norm2_g[0])
    h2 = h2.reshape(B * Lp, D)
    hn2 = hn2.reshape(B * Lp, D)
    gT, eT = _peer_route(hn2, w_peer_q[0], sub_keys1[0], sub_keys2[0])
    uv = jnp.concatenate([expert_u[0], expert_v[0]], axis=1)
    out = _peer_experts(eT.T, gT.T, hn2, h2, uv, norm_f_g)
    return out.reshape(B, Lp, D)[:, TILE:]
```

```python
import functools

import numpy as np
import jax
import jax.numpy as jnp
from jax import lax
from jax.experimental import pallas as pl
from jax.experimental.pallas import tpu as pltpu

F32 = jnp.float32
BF16 = jnp.bfloat16

D_MODEL = 1024
N_META = 16
EPS = 1e-6
TILE = 128
PAD = TILE - N_META

SB_HEADS = 8
SB_DH = 64
SB_W = SB_HEADS * SB_DH
GLA_HEADS = 4
GLA_DK = 64
GLA_DV = 128
GLA_KW = GLA_HEADS * GLA_DK
GLA_W = GLA_HEADS * GLA_DV
GATE_RANK = 16
GLA_TAU = 16.0
GLA_SUB = 16

PEER_HEADS = 8
PEER_NKEYS = 128
PEER_TOPK = 16
PEER_HALF = 64
PEER_K = PEER_HEADS * PEER_TOPK

VMEM_LIMIT = 56 * 1024 * 1024

NT_DIMS = (((1,), (1,)), ((), ()))


def _largest_tile(n_tiles, cap):
    best = 1
    for d in range(1, cap + 1):
        if n_tiles % d == 0:
            best = d
    return best


def _pad_sequence(x, meta_tokens):
    B, _, D = x.shape
    meta = jnp.broadcast_to(meta_tokens.astype(x.dtype)[None], (B, N_META, D))
    return jnp.concatenate([jnp.zeros((B, PAD, D), x.dtype), meta, x], axis=1)


def _inproj_kernel(h_ref, g1_ref, wqT_ref, wk_ref, wvT_ref, wg_ref, wglr_ref, wgu_ref, bgu_ref,
                   qT_ref, k_ref, vT_ref, gq_ref, gk_ref, gv_ref, gr_ref, lg_ref):
    x = h_ref[...]
    ms = jnp.mean(x * x, axis=-1, keepdims=True)
    hn = (x * lax.rsqrt(ms + EPS) * g1_ref[...]).astype(BF16)
    qT_ref[...] = lax.dot_general(wqT_ref[...], hn, NT_DIMS, preferred_element_type=F32).astype(BF16)
    k_ref[...] = jnp.dot(hn, wk_ref[...], preferred_element_type=F32).astype(BF16)
    vT_ref[...] = lax.dot_general(wvT_ref[...], hn, NT_DIMS, preferred_element_type=F32).astype(BF16)
    g = jnp.dot(hn, wg_ref[...], preferred_element_type=F32)
    gq_ref[...] = g[:, 0:GLA_KW]
    gk_ref[...] = g[:, GLA_KW:2 * GLA_KW]
    gv_ref[...] = g[:, 2 * GLA_KW:2 * GLA_KW + GLA_W].astype(BF16)
    gr_ref[...] = g[:, 2 * GLA_KW + GLA_W:]
    glr = jnp.dot(hn, wglr_ref[...], preferred_element_type=F32)
    pre = jnp.dot(glr.astype(BF16), wgu_ref[...], preferred_element_type=F32) + bgu_ref[...]
    lg_ref[...] = (jnp.minimum(pre, 0.0) - jnp.log1p(jnp.exp(-jnp.abs(pre)))) * (1.0 / GLA_TAU)


def _inproj(h_pad, g1, w_in, w_gate_up, b_gate_up):
    B, Lp, D = h_pad.shape
    tm = TILE * _largest_tile(Lp // TILE, 5)
    o = np.cumsum([0, SB_W, SB_W, SB_W, GLA_KW, GLA_KW, GLA_W, GLA_W, GATE_RANK])
    wqT = (w_in[:, o[0]:o[1]] * (SB_DH ** -0.5)).T.astype(BF16)
    wk = w_in[:, o[1]:o[2]].astype(BF16)
    wvT = w_in[:, o[2]:o[3]].T.astype(BF16)
    wg = jnp.concatenate([w_in[:, o[3]:o[4]] * (GLA_DK ** -0.5), w_in[:, o[4]:o[7]]], axis=1).astype(BF16)
    wglr = jnp.pad(w_in[:, o[7]:o[8]], ((0, 0), (0, TILE - GATE_RANK))).astype(BF16)
    wgu = jnp.pad(w_gate_up, ((0, TILE - GATE_RANK), (0, 0))).astype(BF16)
    bgu = b_gate_up.reshape(1, GLA_KW).astype(F32)
    full = lambda shp: pl.BlockSpec(shp, lambda b, i: (0,) * len(shp))
    row = lambda w: pl.BlockSpec((None, tm, w), lambda b, i: (b, i, 0))
    col = lambda w: pl.BlockSpec((None, w, tm), lambda b, i: (b, 0, i))
    out_shape = (
        jax.ShapeDtypeStruct((B, SB_W, Lp), BF16),
        jax.ShapeDtypeStruct((B, Lp, SB_W), BF16),
        jax.ShapeDtypeStruct((B, SB_W, Lp), BF16),
        jax.ShapeDtypeStruct((B, Lp, GLA_KW), F32),
        jax.ShapeDtypeStruct((B, Lp, GLA_KW), F32),
        jax.ShapeDtypeStruct((B, Lp, GLA_W), BF16),
        jax.ShapeDtypeStruct((B, Lp, GLA_W), F32),
        jax.ShapeDtypeStruct((B, Lp, GLA_KW), F32),
    )
    return pl.pallas_call(
        _inproj_kernel,
        out_shape=out_shape,
        grid=(B, Lp // tm),
        in_specs=[row(D), full((1, D)), full((SB_W, D)), full((D, SB_W)), full((SB_W, D)),
                  full((D, 2 * GLA_KW + 2 * GLA_W)), full((D, TILE)), full((TILE, GLA_KW)), full((1, GLA_KW))],
        out_specs=(col(SB_W), row(SB_W), col(SB_W), row(GLA_KW), row(GLA_KW), row(GLA_W), row(GLA_W), row(GLA_KW)),
        compiler_params=pltpu.CompilerParams(dimension_semantics=("parallel", "parallel"),
                                             vmem_limit_bytes=VMEM_LIMIT),
        name="inproj",
    )(h_pad, g1.reshape(1, D).astype(F32), wqT, wk, wvT, wg, wglr, wgu, bgu)


SB_G_MAX = 5
SB_EXIT = -105.0


def _sb_kernel(qT_ref, k_ref, vT_ref, u2_ref, g_ref, oT_ref, *, n_g):
    ig = pl.program_id(2)
    u2 = u2_ref[...]
    top = lax.broadcasted_iota(jnp.int32, (TILE, TILE), 0) < SB_DH
    srow = lax.broadcasted_iota(jnp.int32, (TILE, 2 * TILE), 0)
    lane = lax.broadcasted_iota(jnp.int32, (TILE, 2 * TILE), 1) % TILE
    qpads, tqs = [], []
    for g in range(n_g):
        q = qT_ref[:, g * TILE:(g + 1) * TILE]
        zero = jnp.zeros_like(q)
        qpads.append(jnp.concatenate([jnp.where(top, q, zero), jnp.where(top, zero, q)], axis=1))
        tqs.append(lane + (ig * n_g + g) * TILE)

    def cond(carry):
        s, live = carry[0], carry[1]
        return jnp.logical_and(s <= ig * n_g + n_g - 1, live > SB_EXIT)

    def body(carry):
        s = carry[0]
        os, cs = list(carry[2:2 + n_g]), list(carry[2 + n_g:])
        live = jnp.full((1, 2 * TILE), -jnp.inf, F32)
        for g in range(n_g):
            j = ig * n_g + g - s
            ok = j >= 0
            jc = jnp.maximum(j, 0)
            ks = pl.multiple_of(jc * TILE, TILE)
            z = jnp.dot(k_ref[pl.ds(ks, TILE), :], qpads[g], preferred_element_type=F32)
            sidx = srow + jc * TILE
            valid = (sidx < tqs[g]) & (sidx >= PAD) & ok
            sp = jnp.maximum(z, 0.0) + jnp.log(1.0 + jnp.exp(-jnp.abs(z)))
            spm = jnp.where(valid, sp, 0.0)
            hi = spm.astype(BF16)
            lo = (spm - hi.astype(F32)).astype(BF16)
            gr = jnp.dot(u2, jnp.concatenate([hi, lo], axis=0), preferred_element_type=F32)
            w = z - gr[0:TILE, :] + cs[g]
            a = jnp.where(valid, jnp.exp(w), 0.0).astype(BF16)
            vj = vT_ref[:, pl.ds(ks, TILE)]
            os[g] = os[g] + jnp.concatenate(
                [jnp.dot(vj[0:SB_DH, :], a[:, 0:TILE], preferred_element_type=F32),
                 jnp.dot(vj[SB_DH:, :], a[:, TILE:], preferred_element_type=F32)], axis=0)
            cs[g] = cs[g] - gr[TILE:TILE + 1, :]
            live = jnp.maximum(live, jnp.where(j >= 1, cs[g], -jnp.inf))
        return (s + 1, jnp.max(live), *os, *cs)

    init = (jnp.int32(0), jnp.float32(0.0),
            *[jnp.zeros((TILE, TILE), F32) for _ in range(n_g)],
            *[jnp.zeros((1, 2 * TILE), F32) for _ in range(n_g)])
    res = lax.while_loop(cond, body, init)
    gain = g_ref[...]
    for g in range(n_g):
        o = res[2 + g]
        ms = jnp.concatenate(
            [jnp.broadcast_to(jnp.mean(o[0:SB_DH] * o[0:SB_DH], axis=0, keepdims=True), (SB_DH, TILE)),
             jnp.broadcast_to(jnp.mean(o[SB_DH:] * o[SB_DH:], axis=0, keepdims=True), (SB_DH, TILE))], axis=0)
        oT_ref[:, g * TILE:(g + 1) * TILE] = o * lax.rsqrt(ms + EPS) * gain


def _sb_attention(qT, k, vT, sb_norm_g):
    B, _, Lp = qT.shape
    nq = Lp // TILE
    n_g = _largest_tile(nq, SB_G_MAX)
    u = np.triu(np.ones((TILE, TILE), np.float32))
    u2 = np.concatenate([np.concatenate([u, u], axis=1), np.ones((8, 2 * TILE), np.float32)], axis=0)
    gcol = jnp.broadcast_to(sb_norm_g.reshape(SB_W, 1).astype(F32), (SB_W, TILE))
    return pl.pallas_call(
        functools.partial(_sb_kernel, n_g=n_g),
        out_shape=jax.ShapeDtypeStruct((B, SB_W, Lp), F32),
        grid=(B, SB_HEADS // 2, nq // n_g),
        in_specs=[
            pl.BlockSpec((None, TILE, n_g * TILE), lambda b, p, i: (b, p, i)),
            pl.BlockSpec((None, Lp, TILE), lambda b, p, i: (b, 0, p)),
            pl.BlockSpec((None, TILE, Lp), lambda b, p, i: (b, p, 0)),
            pl.BlockSpec((TILE + 8, 2 * TILE), lambda b, p, i: (0, 0)),
            pl.BlockSpec((TILE, TILE), lambda b, p, i: (p, 0)),
        ],
        out_specs=pl.BlockSpec((None, TILE, n_g * TILE), lambda b, p, i: (b, p, i)),
        compiler_params=pltpu.CompilerParams(dimension_semantics=("parallel", "parallel", "arbitrary"),
                                             vmem_limit_bytes=VMEM_LIMIT),
        name="sb_attention",
    )(qT, k, vT, jnp.asarray(u2, BF16), gcol)


def _gla_kernel(q_ref, k_ref, v_ref, gr_ref, lg_ref, ltc_ref, dsel_ref, gn_ref, o_ref,
                s_scr, bc_scr, attn_scr, plhs_scr):
    c = pl.program_id(1)

    @pl.when(c == 0)
    def _():
        s_scr[...] = jnp.zeros_like(s_scr)

    nsub = TILE // GLA_SUB
    q = q_ref[...]
    k = k_ref[...]
    v = v_ref[...]
    lg = lg_ref[...]
    hi = lg.astype(BF16)
    lo = (lg - hi.astype(F32)).astype(BF16)
    bc_scr[...] = jnp.dot(ltc_ref[...], jnp.concatenate([hi, lo], axis=0), preferred_element_type=F32)
    bcum = bc_scr[0:TILE, :]
    rfull = bc_scr[TILE:2 * TILE, :]
    lane_head = lax.broadcasted_iota(jnp.int32, (1, GLA_KW), 1) // GLA_DK
    hmask = [lane_head == hd for hd in range(GLA_HEADS)]
    rows = lax.broadcasted_iota(jnp.int32, (TILE, GLA_KW), 0)

    qrel = q * jnp.exp(bcum - rfull)
    for hd in range(GLA_HEADS):
        attn_scr[hd, 0:GLA_SUB, :] = jnp.zeros((GLA_SUB, TILE), F32)
    for blk in range(1, nsub):
        r0 = blk * GLA_SUB
        r_b = bc_scr[TILE + r0:TILE + r0 + 1, :]
        ktil = jnp.where(rows < r0, k * jnp.exp(jnp.minimum(r_b - bcum, 0.0)), 0.0).astype(BF16)
        qs = qrel[r0:r0 + GLA_SUB, :]
        qstack = jnp.concatenate([jnp.where(hmask[hd], qs, 0.0) for hd in range(GLA_HEADS)], axis=0).astype(BF16)
        strip = lax.dot_general(qstack, ktil, NT_DIMS, preferred_element_type=F32)
        for hd in range(GLA_HEADS):
            attn_scr[hd, r0:r0 + GLA_SUB, :] = strip[hd * GLA_SUB:(hd + 1) * GLA_SUB, :]

    tl = lax.broadcasted_iota(jnp.int32, (GLA_SUB, GLA_KW), 0)
    for s in range(GLA_SUB):
        parts = []
        for blk in range(nsub):
            r0 = blk * GLA_SUB
            kb = k_ref[r0 + s:r0 + s + 1, :]
            bb = bc_scr[r0 + s:r0 + s + 1, :]
            ok = tl >= s
            dec = jnp.exp(jnp.where(ok, bcum[r0:r0 + GLA_SUB, :] - bb, 0.0))
            parts.append(jnp.where(ok, q[r0:r0 + GLA_SUB, :] * kb * dec, 0.0))
        plhs_scr[:, s * GLA_KW:(s + 1) * GLA_KW] = jnp.concatenate(parts, axis=0).astype(BF16)
    diag = jnp.dot(plhs_scr[...], dsel_ref[...], preferred_element_type=F32)
    trow = lax.broadcasted_iota(jnp.int32, (TILE, GLA_HEADS * TILE), 0) // GLA_SUB
    tcol = (lax.broadcasted_iota(jnp.int32, (TILE, GLA_HEADS * TILE), 1) % TILE) // GLA_SUB
    diag = jnp.where(trow == tcol, diag, 0.0)

    qint = q * jnp.exp(bcum)
    s_old = s_scr[...].astype(BF16)
    gn = gn_ref[...]
    for hd in range(GLA_HEADS):
        cs = slice(hd * GLA_DV, (hd + 1) * GLA_DV)
        attn = (attn_scr[hd] + diag[:, hd * TILE:(hd + 1) * TILE]).astype(BF16)
        o = jnp.dot(attn, v[:, cs], preferred_element_type=F32)
        o = o + jnp.dot(jnp.where(hmask[hd], qint, 0.0).astype(BF16), s_old, preferred_element_type=F32)
        ms = jnp.mean(o * o, axis=-1, keepdims=True)
        g = gr_ref[:, cs]
        o_ref[:, cs] = (o * lax.rsqrt(ms + EPS) * gn[:, cs] * (g / (1.0 + jnp.exp(-g)))).astype(o_ref.dtype)

    b_last = bc_scr[TILE - 1:TILE, :]
    kdec_t = (k * jnp.exp(b_last - bcum)).T.astype(BF16)
    ds = jnp.dot(kdec_t, v, preferred_element_type=F32)
    dec_t = jnp.broadcast_to(jnp.exp(b_last), (TILE, GLA_KW)).T
    for hd in range(GLA_HEADS):
        rs = slice(hd * GLA_DK, (hd + 1) * GLA_DK)
        s_scr[rs, :] = dec_t[rs, :] * s_scr[rs, :] + ds[rs, hd * GLA_DV:(hd + 1) * GLA_DV]


def _gla(gq, gk, gv, gr, lg, gla_norm_g):
    B, Lp, _ = gq.shape
    nsub = TILE // GLA_SUB
    t = np.arange(TILE)
    lt = (t[None, :] <= t[:, None]).astype(np.float32)
    ltsel = (t[None, :] < (t[:, None] // GLA_SUB) * GLA_SUB).astype(np.float32)
    ltc = np.concatenate([np.concatenate([lt, lt], axis=1), np.concatenate([ltsel, ltsel], axis=1)], axis=0)
    dsel = np.zeros((GLA_SUB, GLA_HEADS, GLA_DK, GLA_HEADS, nsub, GLA_SUB), np.float32)
    for s in range(GLA_SUB):
        for hd in range(GLA_HEADS):
            dsel[s, hd, :, hd, :, s] = 1.0
    dsel = dsel.reshape(GLA_SUB * GLA_KW, GLA_HEADS * TILE)
    row = lambda w: pl.BlockSpec((None, TILE, w), lambda b, c: (b, c, 0))
    full = lambda shp: pl.BlockSpec(shp, lambda b, c: (0,) * len(shp))
    return pl.pallas_call(
        _gla_kernel,
        out_shape=jax.ShapeDtypeStruct((B, Lp, GLA_W), BF16),
        grid=(B, Lp // TILE),
        in_specs=[row(GLA_KW), row(GLA_KW), row(GLA_W), row(GLA_W), row(GLA_KW),
                  full((2 * TILE, 2 * TILE)), full((GLA_SUB * GLA_KW, GLA_HEADS * TILE)), full((1, GLA_W))],
        out_specs=row(GLA_W),
        scratch_shapes=[pltpu.VMEM((GLA_KW, GLA_DV), F32), pltpu.VMEM((2 * TILE, GLA_KW), F32),
                        pltpu.VMEM((GLA_HEADS, TILE, TILE), F32), pltpu.VMEM((TILE, GLA_SUB * GLA_KW), BF16)],
        compiler_params=pltpu.CompilerParams(dimension_semantics=("parallel", "arbitrary"),
                                             vmem_limit_bytes=VMEM_LIMIT),
        name="gla",
    )(gq, gk, gv, gr, lg, jnp.asarray(ltc, BF16), jnp.asarray(dsel, BF16), gla_norm_g.reshape(1, GLA_W).astype(F32))


def _outproj_kernel(sbT_ref, gla_ref, h_ref, w_ref, g2_ref, h2_ref, hn2_ref):
    sb = sbT_ref[...].T.astype(BF16)
    y = jnp.dot(sb, w_ref[0:SB_W, :], preferred_element_type=F32)
    y = y + jnp.dot(gla_ref[...], w_ref[SB_W:SB_W + GLA_W, :], preferred_element_type=F32)
    h2 = h_ref[...] + y
    h2_ref[...] = h2
    ms = jnp.mean(h2 * h2, axis=-1, keepdims=True)
    hn2_ref[...] = h2 * lax.rsqrt(ms + EPS) * g2_ref[...]


def _outproj(sb_oT, gla_o, h_pad, w_out, g2):
    B, Lp, D = h_pad.shape
    tm = TILE * _largest_tile(Lp // TILE, 5)
    row = lambda w: pl.BlockSpec((None, tm, w), lambda b, i: (b, i, 0))
    full = lambda shp: pl.BlockSpec(shp, lambda b, i: (0,) * len(shp))
    return pl.pallas_call(
        _outproj_kernel,
        out_shape=(jax.ShapeDtypeStruct((B, Lp, D), F32), jax.ShapeDtypeStruct((B, Lp, D), F32)),
        grid=(B, Lp // tm),
        in_specs=[pl.BlockSpec((None, SB_W, tm), lambda b, i: (b, 0, i)), row(GLA_W), row(D),
                  full((SB_W + GLA_W, D)), full((1, D))],
        out_specs=(row(D), row(D)),
        compiler_params=pltpu.CompilerParams(dimension_semantics=("parallel", "parallel"),
                                             vmem_limit_bytes=VMEM_LIMIT),
        name="outproj",
    )(sb_oT, gla_o, h_pad, w_out.astype(BF16), g2.reshape(1, D).astype(F32))


ROUTE_TT = 256


def _top16_rows(s, n_rows, val_scr, idx_scr):
    riota = lax.broadcasted_iota(jnp.int32, s.shape, 0)
    for r in range(PEER_TOPK):
        m = jnp.max(s, axis=0, keepdims=True)
        idx = jnp.min(jnp.where(s == m, riota, n_rows), axis=0, keepdims=True)
        val_scr[r:r + 1, :] = m
        idx_scr[r:r + 1, :] = idx
        s = jnp.where(riota == idx, -jnp.inf, s)


def _route_kernel(x_ref, wqT_ref, sk1_ref, sk2_ref, gT_ref, eT_ref,
                  qT_scr, v1_scr, i1_scr, v2_scr, i2_scr, cand_scr, sc_scr, pos_scr):
    tt = x_ref.shape[0]
    qT_scr[...] = lax.dot_general(wqT_ref[...], x_ref[...].astype(BF16), NT_DIMS,
                                  preferred_element_type=F32).astype(BF16)
    k16 = lax.broadcasted_iota(jnp.int32, (PEER_TOPK, TILE), 0)

    def head_body(hd, carry):
        r1 = pl.multiple_of(hd * 2 * PEER_HALF, 2 * PEER_HALF)
        r2 = pl.multiple_of(hd * 2 * PEER_HALF + PEER_HALF, PEER_HALF)
        ro = pl.multiple_of(hd * PEER_TOPK, PEER_TOPK)
        for lb in range(tt // TILE):
            ls = slice(lb * TILE, (lb + 1) * TILE)
            s1 = jnp.dot(sk1_ref[hd], qT_scr[pl.ds(r1, PEER_HALF), ls], preferred_element_type=F32)
            s2 = jnp.dot(sk2_ref[hd], qT_scr[pl.ds(r2, PEER_HALF), ls], preferred_element_type=F32)
            _top16_rows(s1, PEER_NKEYS, v1_scr, i1_scr)
            _top16_rows(s2, PEER_NKEYS, v2_scr, i2_scr)
            v2 = v2_scr[...]
            for a in range(PEER_TOPK):
                cand_scr[a * PEER_TOPK:(a + 1) * PEER_TOPK, :] = v1_scr[a:a + 1, :] + v2
            _top16_rows(cand_scr[...], PEER_TOPK * PEER_TOPK, sc_scr, pos_scr)
            pos = pos_scr[...]
            i1 = i1_scr[...]
            i2 = i2_scr[...]
            e_rows = []
            for r in range(PEER_TOPK):
                pa = pos[r:r + 1, :] // PEER_TOPK
                pb = pos[r:r + 1, :] % PEER_TOPK
                e1 = jnp.sum(jnp.where(k16 == pa, i1, 0), axis=0, keepdims=True)
                e2 = jnp.sum(jnp.where(k16 == pb, i2, 0), axis=0, keepdims=True)
                e_rows.append(e1 * PEER_NKEYS + e2)
            sc = sc_scr[...]
            ex = jnp.exp(sc - sc[0:1, :])
            gT_ref[pl.ds(ro, PEER_TOPK), ls] = ex / jnp.sum(ex, axis=0, keepdims=True)
            eT_ref[pl.ds(ro, PEER_TOPK), ls] = jnp.concatenate(e_rows, axis=0)
        return carry

    lax.fori_loop(0, PEER_HEADS, head_body, 0)


def _peer_route(hn2, w_peer_q, sub_keys1, sub_keys2):
    Tp, D = hn2.shape
    tt = ROUTE_TT
    full = lambda shp: pl.BlockSpec(shp, lambda i: (0,) * len(shp))
    sm = lambda dt: pltpu.VMEM((PEER_TOPK, TILE), dt)
    return pl.pallas_call(
        _route_kernel,
        out_shape=(jax.ShapeDtypeStruct((PEER_K, Tp), F32), jax.ShapeDtypeStruct((PEER_K, Tp), jnp.int32)),
        grid=(Tp // tt,),
        in_specs=[pl.BlockSpec((tt, D), lambda i: (i, 0)), full((D, D)),
                  full((PEER_HEADS, PEER_NKEYS, PEER_HALF)), full((PEER_HEADS, PEER_NKEYS, PEER_HALF))],
        out_specs=(pl.BlockSpec((PEER_K, tt), lambda i: (0, i)), pl.BlockSpec((PEER_K, tt), lambda i: (0, i))),
        scratch_shapes=[pltpu.VMEM((D, tt), BF16), sm(F32), sm(jnp.int32), sm(F32), sm(jnp.int32),
                        pltpu.VMEM((PEER_TOPK * PEER_TOPK, TILE), F32), sm(F32), sm(jnp.int32)],
        compiler_params=pltpu.CompilerParams(dimension_semantics=("parallel",), vmem_limit_bytes=VMEM_LIMIT),
        name="peer_route",
    )(hn2, w_peer_q.T.astype(BF16), sub_keys1.astype(BF16), sub_keys2.astype(BF16))


EXP_TB = 128
EXP_NBUF = 8
EXP_ROWS = 8
PACK_ROWS = 512


def _pack_kernel(u_ref, v_ref, w_ref):
    ub = pltpu.bitcast(u_ref[...].astype(BF16).astype(F32), jnp.uint32)
    vb = pltpu.bitcast(v_ref[...].astype(BF16).astype(F32), jnp.uint32)
    w_ref[...] = (ub >> 16) | (vb & jnp.uint32(0xFFFF0000))


def _pack_experts(expert_u, expert_v):
    n, d = expert_u.shape
    blk = pl.BlockSpec((PACK_ROWS, d), lambda i: (i, 0))
    return pl.pallas_call(
        _pack_kernel,
        out_shape=jax.ShapeDtypeStruct((n, d), jnp.uint32),
        grid=(n // PACK_ROWS,),
        in_specs=[blk, blk],
        out_specs=blk,
        compiler_params=pltpu.CompilerParams(dimension_semantics=("parallel",), vmem_limit_bytes=VMEM_LIMIT),
        name="pack_experts",
    )(expert_u, expert_v)


def _experts_kernel(ecur_ref, enxt_ref, gates_ref, x_ref, h2_ref, gf_ref, tab_hbm, o_ref, buf, sem):
    step = pl.program_id(0)
    last = pl.num_programs(0) - 1
    tb = x_ref.shape[0]
    look = EXP_NBUF - 1
    per_chunk = PEER_K // EXP_ROWS
    eye = (lax.broadcasted_iota(jnp.int32, (PEER_K, PEER_K), 0)
           == lax.broadcasted_iota(jnp.int32, (PEER_K, PEER_K), 1))
    himask = jnp.uint32(0xFFFF0000)
    n_words = EXP_ROWS * TILE

    def issue_rows(e_ref, t, slot, lo, hi):
        for kk in range(lo, hi):
            e = e_ref[t, kk]
            pltpu.make_async_copy(tab_hbm.at[e], buf.at[slot, pl.ds(kk * EXP_ROWS, EXP_ROWS), :],
                                  sem.at[slot]).start(priority=kk % 2)

    def wait_slot(slot):
        pltpu.make_async_copy(buf.at[slot], buf.at[slot], sem.at[slot]).wait()

    def process(t, slot, issue_chunk):
        wait_slot(slot)
        x = x_ref[t]
        n_grp = PEER_K // EXP_ROWS
        issued = 0
        acts = []
        for r in range(n_grp):
            acc = jnp.zeros((EXP_ROWS, TILE), F32)
            for j in range(EXP_ROWS):
                if j % 2 == 0:
                    issue_chunk(issued, issued + 1)
                    issued += 1
                w = buf[slot, pl.ds(r * EXP_ROWS * EXP_ROWS + j, EXP_ROWS, stride=EXP_ROWS), :]
                acc = acc + pltpu.bitcast(w << 16, F32) * x[j:j + 1, :]
            acts.append(jnp.sum(acc, axis=1, keepdims=True))
        act = jnp.concatenate(acts, axis=0)
        gcol = jnp.sum(jnp.where(eye, gates_ref[pl.ds(t, 1), :], 0.0), axis=1, keepdims=True)
        cw = gcol * (0.5 * act * (1.0 + lax.erf(act * np.float32(2.0 ** -0.5))))
        ys = [jnp.zeros((EXP_ROWS, TILE), F32) for _ in range(EXP_ROWS)]
        for r in range(n_grp):
            cwr = jnp.broadcast_to(cw[r * EXP_ROWS:(r + 1) * EXP_ROWS, :], (EXP_ROWS, TILE))
            for j in range(EXP_ROWS):
                if j % 2 == 0:
                    issue_chunk(issued, issued + 1)
                    issued += 1
                w = buf[slot, pl.ds(r * EXP_ROWS * EXP_ROWS + j, EXP_ROWS, stride=EXP_ROWS), :]
                ys[j] = ys[j] + cwr * pltpu.bitcast(w & himask, F32)
        assert issued == PEER_K
        rows = [jnp.sum(yj, axis=0, keepdims=True) for yj in ys]
        hh = h2_ref[t] + jnp.concatenate(rows, axis=0)
        ms = jnp.sum(jnp.sum(hh * hh, axis=1, keepdims=True), axis=0, keepdims=True) * (1.0 / n_words)
        o_ref[t] = hh * lax.rsqrt(ms + EPS) * gf_ref[...]

    @pl.when(step == 0)
    def _():
        for p in range(look):
            issue_rows(ecur_ref, p, p, 0, PEER_K)

    def group(base, last_group):
        for p in range(EXP_NBUF):
            tn = base + p + look
            nslot = (p + look) % EXP_NBUF
            if last_group and p >= 1:
                chunk = functools.partial(issue_rows, enxt_ref, p - 1, nslot)
            else:
                chunk = functools.partial(issue_rows, ecur_ref, tn, nslot)
            process(base + p, p, chunk)

    def main(g, carry):
        group(pl.multiple_of(g * EXP_NBUF, EXP_NBUF), False)
        return carry

    lax.fori_loop(0, tb // EXP_NBUF - 1, main, 0)
    group(tb - EXP_NBUF, True)

    @pl.when(step == last)
    def _():
        for p in range(look):
            wait_slot(p)


def _peer_experts(eidx, gates, hn2, h2, table, gf):
    Tp, D = hn2.shape
    tb = EXP_TB
    assert tb % EXP_NBUF == 0 and Tp % tb == 0 and D == EXP_ROWS * TILE
    n_steps = Tp // tb
    tile3 = pl.BlockSpec((tb, EXP_ROWS, TILE), lambda i: (i, 0, 0))
    out = pl.pallas_call(
        _experts_kernel,
        out_shape=jax.ShapeDtypeStruct((Tp, EXP_ROWS, TILE), F32),
        grid=(n_steps,),
        in_specs=[pl.BlockSpec((tb, PEER_K), lambda i: (i, 0), memory_space=pltpu.SMEM),
                  pl.BlockSpec((tb, PEER_K), lambda i: (jnp.minimum(i + 1, n_steps - 1), 0), memory_space=pltpu.SMEM),
                  pl.BlockSpec((tb, PEER_K), lambda i: (i, 0)), tile3, tile3,
                  pl.BlockSpec((EXP_ROWS, TILE), lambda i: (0, 0)),
                  pl.BlockSpec(memory_space=pl.ANY)],
        out_specs=tile3,
        scratch_shapes=[pltpu.VMEM((EXP_NBUF, PEER_K * EXP_ROWS, TILE), jnp.uint32),
                        pltpu.SemaphoreType.DMA((EXP_NBUF,))],
        compiler_params=pltpu.CompilerParams(dimension_semantics=("arbitrary",), vmem_limit_bytes=VMEM_LIMIT),
        name="peer_experts",
    )(eidx, eidx, gates, hn2.reshape(Tp, EXP_ROWS, TILE), h2.reshape(Tp, EXP_ROWS, TILE),
      gf.reshape(EXP_ROWS, TILE).astype(F32), table.reshape(table.shape[0], EXP_ROWS, TILE))
    return out.reshape(Tp, D)


def kernel(x, meta_tokens, norm1_g, w_in, w_gate_up, b_gate_up, sb_norm_g, gla_norm_g, w_out,
           norm2_g, w_peer_q, sub_keys1, sub_keys2, expert_u, expert_v, norm_f_g):
    B, S, D = x.shape
    Lp = S + TILE
    h_pad = _pad_sequence(x, meta_tokens)
    qT, k, vT, gq, gk, gv, gr, lg = _inproj(h_pad, norm1_g[0], w_in[0], w_gate_up[0], b_gate_up[0])
    sb_oT = _sb_attention(qT, k, vT, sb_norm_g[0])
    gla_o = _gla(gq, gk, gv, gr, lg, gla_norm_g[0])
    h2, hn2 = _outproj(sb_oT, gla_o, h_pad, w_out[0], norm2_g[0])
    h2 = h2.reshape(B * Lp, D)
    hn2 = hn2.reshape(B * Lp, D)
    gT, eT = _peer_route(hn2, w_peer_q[0], sub_keys1[0], sub_keys2[0])
    table = _pack_experts(expert_u[0], expert_v[0])
    out = _peer_experts(eT.T, gT.T, hn2, h2, table, norm_f_g)
    return out.reshape(B, Lp, D)[:, TILE:]
```
